```python
import math
import jax
import jax.numpy as jnp
from jax import lax
import numpy as np

D_MODEL = 1024
BATCH = 8
SEQ = 4096
DEPTH = 2

N_MIXERS = 2
N_POOL_LAYERS = (DEPTH + 1) // 2
N_ATTN_LAYERS = DEPTH // 2

POOL_WINDOWS = (2, 4, 8, 16)
N_POOL_GROUPS = len(POOL_WINDOWS)
POOL_GC = D_MODEL // N_POOL_GROUPS

HEAD_DIM = 64
N_HEADS = D_MODEL // HEAD_DIM
N_KV_HEADS = 4
N_GROUPS = N_HEADS // N_KV_HEADS
WINDOW = 128
BLOCK = 128
QKV_DIM = (N_HEADS + 2 * N_KV_HEADS) * HEAD_DIM

D_FF = 2816
CONV_W = 3

RMS_EPS = 1e-6

kernel_name = "hybrid_pool_swa_sink_alibi_convffn"


def rms_norm(x, g):
    xf = x.astype(jnp.float32)
    y = xf * lax.rsqrt(jnp.mean(xf * xf, axis=-1, keepdims=True) + RMS_EPS)
    return (y * g.astype(jnp.float32)).astype(x.dtype)


def alibi_slopes(n_heads):
    h = jnp.arange(1, n_heads + 1, dtype=jnp.float32)
    return jnp.exp2(-8.0 / n_heads * h)


def pool_mixer(h, w, b, scale):
    B, S, D = h.shape
    hf = h.astype(jnp.float32)
    csum = jnp.cumsum(hf, axis=1)
    t1 = jnp.arange(1, S + 1)
    outs = []
    for gi, win in enumerate(POOL_WINDOWS):
        sl = slice(gi * POOL_GC, (gi + 1) * POOL_GC)
        c = csum[..., sl]
        lo = jnp.pad(c, ((0, 0), (win, 0), (0, 0)))[:, :S]
        count = jnp.minimum(t1, win).astype(jnp.float32)
        mean = (c - lo) / count[None, :, None]
        outs.append(mean - hf[..., sl])
    p = jnp.stack(outs, axis=2).astype(h.dtype)
    y = jnp.einsum('bsgc,gcd->bsgd', p, w) + b
    return y.reshape(B, S, D) * scale


def swa_sink_attention(h, w_qkv, b_qkv, sinks, w_o):
    B, S, _ = h.shape
    nb = S // BLOCK
    qkv = h @ w_qkv + b_qkv
    q = qkv[..., :N_HEADS * HEAD_DIM].reshape(B, nb, BLOCK, N_KV_HEADS, N_GROUPS, HEAD_DIM)
    k = qkv[..., N_HEADS * HEAD_DIM:(N_HEADS + N_KV_HEADS) * HEAD_DIM].reshape(B, S, N_KV_HEADS, HEAD_DIM)
    v = qkv[..., (N_HEADS + N_KV_HEADS) * HEAD_DIM:].reshape(B, S, N_KV_HEADS, HEAD_DIM)

    def band(t):
        tp = jnp.pad(t, ((0, 0), (BLOCK, 0), (0, 0), (0, 0)))[:, :S]
        prev = tp.reshape(B, nb, BLOCK, N_KV_HEADS, HEAD_DIM)
        cur = t.reshape(B, nb, BLOCK, N_KV_HEADS, HEAD_DIM)
        return jnp.concatenate([prev, cur], axis=2)

    kb, vb = band(k), band(v)
    scores = jnp.einsum('bnqkgd,bnskd->bnkgqs', q, kb).astype(jnp.float32) * (HEAD_DIM ** -0.5)

    qi = jnp.arange(BLOCK)[:, None]
    kj = jnp.arange(2 * BLOCK)[None, :]
    dist = qi + BLOCK - kj
    key_pos = jnp.arange(nb)[:, None] * BLOCK - BLOCK + jnp.arange(2 * BLOCK)[None, :]
    mask = ((dist >= 0) & (dist < WINDOW))[None] & (key_pos >= 0)[:, None, :]

    slopes = alibi_slopes(N_HEADS).reshape(N_KV_HEADS, N_GROUPS)
    bias = -slopes[:, :, None, None] * dist.astype(jnp.float32)[None, None]
    scores = jnp.where(mask[None, :, None, None], scores + bias, -jnp.inf)

    sink = sinks.astype(jnp.float32).reshape(N_KV_HEADS, N_GROUPS)[None, None, :, :, None, None]
    m = jnp.maximum(jnp.max(scores, axis=-1, keepdims=True), sink)
    p = jnp.exp(scores - m)
    denom = jnp.sum(p, axis=-1, keepdims=True) + jnp.exp(sink - m)
    probs = (p / denom).astype(v.dtype)
    o = jnp.einsum('bnkgqs,bnskd->bnqkgd', probs, vb).reshape(B, S, N_HEADS * HEAD_DIM)
    return o @ w_o


def conv_glu_ffn(h, w_in, conv_w, conv_b, w_out):
    u = h @ w_in
    c = lax.conv_general_dilated(
        u, conv_w[:, None, :].astype(u.dtype), window_strides=(1,),
        padding=[(CONV_W - 1, 0)], dimension_numbers=('NWC', 'WIO', 'NWC'),
        feature_group_count=2 * D_FF) + conv_b
    a, g = c[..., :D_FF], c[..., D_FF:]
    return (a * jax.nn.silu(g)) @ w_out


def setup_inputs(seed: int = 0) -> dict:
    key = jax.random.key(seed)
    ks = jax.random.split(key, 16)
    f32 = jnp.float32
    nrm = lambda k, shape, s: jax.random.normal(k, shape, f32) * s
    return {
        "x": nrm(ks[0], (BATCH, SEQ, D_MODEL), 1.0),
        "pool_w": nrm(ks[1], (N_POOL_LAYERS, N_POOL_GROUPS, POOL_GC, POOL_GC), POOL_GC ** -0.5),
        "pool_b": nrm(ks[2], (N_POOL_LAYERS, N_POOL_GROUPS, POOL_GC), 0.02),
        "pool_scale": 1.0 + nrm(ks[3], (N_POOL_LAYERS, D_MODEL), 0.05),
        "attn_w_qkv": nrm(ks[4], (N_ATTN_LAYERS, D_MODEL, QKV_DIM), D_MODEL ** -0.5),
        "attn_b_qkv": nrm(ks[5], (N_ATTN_LAYERS, QKV_DIM), 0.02),
        "attn_sinks": nrm(ks[6], (N_ATTN_LAYERS, N_HEADS), 0.5),
        "attn_w_o": nrm(ks[7], (N_ATTN_LAYERS, N_HEADS * HEAD_DIM, D_MODEL), (N_HEADS * HEAD_DIM) ** -0.5),
        "norm_mix": 1.0 + nrm(ks[8], (DEPTH, D_MODEL), 0.05),
        "norm_ffn": 1.0 + nrm(ks[9], (DEPTH, D_MODEL), 0.05),
        "ffn_w_in": nrm(ks[10], (DEPTH, D_MODEL, 2 * D_FF), D_MODEL ** -0.5),
        "ffn_conv_w": nrm(ks[11], (DEPTH, CONV_W, 2 * D_FF), CONV_W ** -0.5),
        "ffn_conv_b": nrm(ks[12], (DEPTH, 2 * D_FF), 0.02),
        "ffn_w_out": nrm(ks[13], (DEPTH, D_FF, D_MODEL), D_FF ** -0.5),
        "norm_f": 1.0 + nrm(ks[14], (D_MODEL,), 0.05),
    }


def reference(x, pool_w, pool_b, pool_scale, attn_w_qkv, attn_b_qkv, attn_sinks, attn_w_o,
              norm_mix, norm_ffn, ffn_w_in, ffn_conv_w, ffn_conv_b, ffn_w_out, norm_f):
    h = x
    for i in range(DEPTH):
        hn = rms_norm(h, norm_mix[i])
        j = i // N_MIXERS
        if i % N_MIXERS == 0:
            mix = pool_mixer(hn, pool_w[j], pool_b[j], pool_scale[j])
        else:
            mix = swa_sink_attention(hn, attn_w_qkv[j], attn_b_qkv[j], attn_sinks[j], attn_w_o[j])
        h = h + mix
        hn = rms_norm(h, norm_ffn[i])
        h = h + conv_glu_ffn(hn, ffn_w_in[i], ffn_conv_w[i], ffn_conv_b[i], ffn_w_out[i])
    return rms_norm(h, norm_f)
```

```python
import functools
import math

import jax
import jax.numpy as jnp
from jax import lax
from jax.experimental import pallas as pl
from jax.experimental.pallas import tpu as pltpu

RMS_EPS = 1e-6
POOL_WINDOWS = (2, 4, 8, 16)
POOL_HALO = 16
HEAD_DIM = 64
N_HEADS = 16
N_KV_HEADS = 4
N_GROUPS = N_HEADS // N_KV_HEADS
ATTN_BLOCK = 128
CONV_W = 3
SUBLANES = 8

SEQ_TILE = 512
FFN_CHUNK = 256
VMEM_LIMIT_BYTES = 56 * 1024 * 1024


def _rms_norm(x, g):
    ms = jnp.mean(x * x, axis=-1, keepdims=True)
    return x * lax.rsqrt(ms + RMS_EPS) * g


def _compiler_params():
    return pltpu.CompilerParams(
        dimension_semantics=("arbitrary", "arbitrary"),
        vmem_limit_bytes=VMEM_LIMIT_BYTES,
    )


def _resident(shape):
    return pl.BlockSpec(shape, lambda b, s: (0,) * len(shape),
                        pipeline_mode=pl.Buffered(1))


def _seq_tile_spec(tile, d):
    return pl.BlockSpec((None, tile, d), lambda b, s: (b, s, 0))


def _pool_kernel(h_ref, g_ref, w_ref, b_ref, sc_ref, o_ref, halo_ref, *, tile):
    s = pl.program_id(1)

    @pl.when(s == 0)
    def _():
        halo_ref[...] = jnp.zeros_like(halo_ref)

    h = h_ref[...]
    hn = _rms_norm(h, g_ref[...])
    ext = jnp.concatenate([halo_ref[...], hn], axis=0)
    halo_ref[...] = hn[tile - POOL_HALO:, :]

    gc = w_ref.shape[1]
    pos1 = lax.broadcasted_iota(jnp.int32, (tile, gc), 0) + (s * tile + 1)
    for gi, win in enumerate(POOL_WINDOWS):
        sl = slice(gi * gc, (gi + 1) * gc)
        acc = ext[:, sl]
        k = 1
        while k < win:
            acc = acc + pltpu.roll(acc, k, axis=0)
            k *= 2
        count = jnp.minimum(pos1, win).astype(jnp.float32)
        p = acc[POOL_HALO:, :] / count - hn[:, sl]
        y = jnp.dot(p.astype(jnp.bfloat16), w_ref[gi],
                    preferred_element_type=jnp.float32) + b_ref[:, sl]
        o_ref[:, sl] = h[:, sl] + y * sc_ref[:, sl]


def _pool_layer(h, g, w, b, scale):
    bsz, seq, d = h.shape
    tile = SEQ_TILE
    ng, gc, _ = w.shape
    return pl.pallas_call(
        functools.partial(_pool_kernel, tile=tile),
        grid=(bsz, seq // tile),
        in_specs=[
            _seq_tile_spec(tile, d),
            _resident((1, d)),
            _resident((ng, gc, gc)),
            _resident((1, d)),
            _resident((1, d)),
        ],
        out_specs=_seq_tile_spec(tile, d),
        out_shape=jax.ShapeDtypeStruct(h.shape, h.dtype),
        scratch_shapes=[pltpu.VMEM((POOL_HALO, d), jnp.float32)],
        compiler_params=_compiler_params(),
        name="pool_mixer",
    )(h, g.reshape(1, d), w.astype(jnp.bfloat16), b.reshape(1, d), scale.reshape(1, d))


def _ffn_kernel(h_ref, g_ref, win_ref, cw_ref, cb_ref, wout_ref, gf_ref, o_ref,
                u_ref, carry_ref, *, tile, d_ff, final_norm):
    s = pl.program_id(1)

    @pl.when(s == 0)
    def _():
        carry_ref[...] = jnp.zeros_like(carry_ref)

    h = h_ref[...]
    hn = _rms_norm(h, g_ref[...]).astype(jnp.bfloat16)

    def conv_branch(col):
        cs = slice(col, col + FFN_CHUNK)
        u = jnp.dot(hn, win_ref[:, cs], preferred_element_type=jnp.float32)
        u_ref[0:SUBLANES, :] = carry_ref[:, cs]
        u_ref[SUBLANES:, :] = u
        carry_ref[:, cs] = u[tile - SUBLANES:, :]
        c = (u * cw_ref[2:3, cs]
             + u_ref[SUBLANES - 1:SUBLANES - 1 + tile, :] * cw_ref[1:2, cs]
             + u_ref[SUBLANES - 2:SUBLANES - 2 + tile, :] * cw_ref[0:1, cs])
        return c + cb_ref[:, cs]

    acc = h
    for j in range(d_ff // FFN_CHUNK):
        a = conv_branch(j * FFN_CHUNK)
        gate = conv_branch(d_ff + j * FFN_CHUNK)
        act = a * (gate / (1.0 + jnp.exp(-gate)))
        acc = acc + jnp.dot(act.astype(jnp.bfloat16),
                            wout_ref[j * FFN_CHUNK:(j + 1) * FFN_CHUNK, :],
                            preferred_element_type=jnp.float32)
    if final_norm:
        acc = _rms_norm(acc, gf_ref[...])
    o_ref[...] = acc


def _ffn_layer(h, g, w_in, conv_w, conv_b, w_out, g_final=None):
    bsz, seq, d = h.shape
    tile = SEQ_TILE
    d_ff = w_out.shape[0]
    final_norm = g_final is not None
    gf = g_final if final_norm else jnp.ones((d,), h.dtype)
    return pl.pallas_call(
        functools.partial(_ffn_kernel, tile=tile, d_ff=d_ff, final_norm=final_norm),
        grid=(bsz, seq // tile),
        in_specs=[
            _seq_tile_spec(tile, d),
            _resident((1, d)),
            _resident((d, 2 * d_ff)),
            _resident((CONV_W, 2 * d_ff)),
            _resident((1, 2 * d_ff)),
            _resident((d_ff, d)),
            _resident((1, d)),
        ],
        out_specs=_seq_tile_spec(tile, d),
        out_shape=jax.ShapeDtypeStruct(h.shape, h.dtype),
        scratch_shapes=[
            pltpu.VMEM((tile + SUBLANES, FFN_CHUNK), jnp.float32),
            pltpu.VMEM((SUBLANES, 2 * d_ff), jnp.float32),
        ],
        compiler_params=_compiler_params(),
        name="conv_glu_ffn",
    )(h, g.reshape(1, d), w_in.astype(jnp.bfloat16), conv_w,
      conv_b.reshape(1, 2 * d_ff), w_out.astype(jnp.bfloat16), gf.reshape(1, d))


def _attn_kernel(sink_ref, h_ref, g_ref, wqkv_ref, bqkv_ref, wo_ref, o_ref,
                 kv_ref, att_ref, *, tile):
    s = pl.program_id(1)
    blk = ATTN_BLOCK
    nq = N_HEADS * HEAD_DIM
    nkv = N_KV_HEADS * HEAD_DIM

    @pl.when(s == 0)
    def _():
        kv_ref[0:blk, :] = jnp.zeros((blk, 2 * nkv), kv_ref.dtype)

    h = h_ref[...]
    hn = _rms_norm(h, g_ref[...]).astype(jnp.bfloat16)
    qkv = jnp.dot(hn, wqkv_ref[...], preferred_element_type=jnp.float32) + bqkv_ref[...]
    q = (qkv[:, :nq] * (HEAD_DIM ** -0.5)).astype(jnp.bfloat16)
    kv_ref[blk:, :] = qkv[:, nq:].astype(kv_ref.dtype)

    qi = lax.broadcasted_iota(jnp.int32, (blk, 2 * blk), 0)
    kj = lax.broadcasted_iota(jnp.int32, (blk, 2 * blk), 1)
    dist = qi + blk - kj
    band = (dist >= 0) & (dist < blk)
    dist_f = dist.astype(jnp.float32)

    for n in range(tile // blk):
        first_block = (s * (tile // blk) + n) == 0
        mask = band & ((kj >= blk) | jnp.logical_not(first_block))
        rows = slice(n * blk, (n + 1) * blk)
        for kh in range(N_KV_HEADS):
            kb = kv_ref[n * blk:(n + 2) * blk, kh * HEAD_DIM:(kh + 1) * HEAD_DIM]
            vb = kv_ref[n * blk:(n + 2) * blk, nkv + kh * HEAD_DIM:nkv + (kh + 1) * HEAD_DIM]
            for gq in range(N_GROUPS):
                hd = kh * N_GROUPS + gq
                cols = slice(hd * HEAD_DIM, (hd + 1) * HEAD_DIM)
                slope = math.pow(2.0, -8.0 / N_HEADS * (hd + 1))
                sc = lax.dot_general(q[rows, cols], kb, (((1,), (1,)), ((), ())),
                                     preferred_element_type=jnp.float32)
                sc = jnp.where(mask, sc - slope * dist_f, -jnp.inf)
                sink = sink_ref[hd]
                m = jnp.maximum(jnp.max(sc, axis=-1, keepdims=True), sink)
                p = jnp.exp(sc - m)
                denom = jnp.sum(p, axis=-1, keepdims=True) + jnp.exp(sink - m)
                pv = jnp.dot(p.astype(jnp.bfloat16), vb, preferred_element_type=jnp.float32)
                att_ref[rows, cols] = (pv / denom).astype(att_ref.dtype)

    kv_ref[0:blk, :] = kv_ref[tile:tile + blk, :]
    o_ref[...] = h + jnp.dot(att_ref[...], wo_ref[...], preferred_element_type=jnp.float32)


def _attn_layer(h, g, w_qkv, b_qkv, sinks, w_o):
    bsz, seq, d = h.shape
    tile = SEQ_TILE
    nq = N_HEADS * HEAD_DIM
    nkv = N_KV_HEADS * HEAD_DIM
    qkv_dim = nq + 2 * nkv
    return pl.pallas_call(
        functools.partial(_attn_kernel, tile=tile),
        grid=(bsz, seq // tile),
        in_specs=[
            pl.BlockSpec(memory_space=pltpu.SMEM),
            _seq_tile_spec(tile, d),
            _resident((1, d)),
            _resident((d, qkv_dim)),
            _resident((1, qkv_dim)),
            _resident((nq, d)),
        ],
        out_specs=_seq_tile_spec(tile, d),
        out_shape=jax.ShapeDtypeStruct(h.shape, h.dtype),
        scratch_shapes=[
            pltpu.VMEM((tile + ATTN_BLOCK, 2 * nkv), jnp.bfloat16),
            pltpu.VMEM((tile, nq), jnp.bfloat16),
        ],
        compiler_params=_compiler_params(),
        name="swa_sink_attention",
    )(sinks.astype(jnp.float32), h, g.reshape(1, d), w_qkv.astype(jnp.bfloat16),
      b_qkv.reshape(1, qkv_dim), w_o.astype(jnp.bfloat16))


def kernel(x, pool_w, pool_b, pool_scale, attn_w_qkv, attn_b_qkv, attn_sinks, attn_w_o,
           norm_mix, norm_ffn, ffn_w_in, ffn_conv_w, ffn_conv_b, ffn_w_out, norm_f):
    depth = norm_mix.shape[0]
    h = x
    for i in range(depth):
        j = i // 2
        if i % 2 == 0:
            h = _pool_layer(h, norm_mix[i], pool_w[j], pool_b[j], pool_scale[j])
        else:
            h = _attn_layer(h, norm_mix[i], attn_w_qkv[j], attn_b_qkv[j],
                            attn_sinks[j], attn_w_o[j])
        h = _ffn_layer(h, norm_ffn[i], ffn_w_in[i], ffn_conv_w[i], ffn_conv_b[i],
                       ffn_w_out[i], g_final=norm_f if i == depth - 1 else None)
    return h
```

```python
import functools
import math

import jax
import jax.numpy as jnp
from jax import lax
from jax.experimental import pallas as pl
from jax.experimental.pallas import tpu as pltpu

RMS_EPS = 1e-6
POOL_WINDOWS = (2, 4, 8, 16)
POOL_HALO = 16
HEAD_DIM = 64
N_HEADS = 16
N_KV_HEADS = 4
N_GROUPS = N_HEADS // N_KV_HEADS
ATTN_BLOCK = 128
CONV_W = 3
SUBLANES = 8

SEQ_TILE = 512
FFN_CHUNK = 256
FFN_SUB_TILE = 256
FFN_ROW_BLOCKS = 4
FFN_DOWN_DELAY = 2
VMEM_LIMIT_BYTES = 56 * 1024 * 1024


def _rms_norm(x, g):
    ms = jnp.mean(x * x, axis=-1, keepdims=True)
    return x * lax.rsqrt(ms + RMS_EPS) * g


def _compiler_params():
    return pltpu.CompilerParams(
        dimension_semantics=("arbitrary", "arbitrary"),
        vmem_limit_bytes=VMEM_LIMIT_BYTES,
    )


def _resident(shape):
    return pl.BlockSpec(shape, lambda b, s: (0,) * len(shape),
                        pipeline_mode=pl.Buffered(1))


def _seq_tile_spec(tile, d):
    return pl.BlockSpec((None, tile, d), lambda b, s: (b, s, 0))


def _pool_kernel(h_ref, g_ref, w_ref, b_ref, sc_ref, o_ref, halo_ref, *, tile):
    s = pl.program_id(1)

    @pl.when(s == 0)
    def _():
        halo_ref[...] = jnp.zeros_like(halo_ref)

    h = h_ref[...]
    hn = _rms_norm(h, g_ref[...])
    ext = jnp.concatenate([halo_ref[...], hn], axis=0)
    halo_ref[...] = hn[tile - POOL_HALO:, :]

    gc = w_ref.shape[1]
    pos1 = lax.broadcasted_iota(jnp.int32, (tile, gc), 0) + (s * tile + 1)
    for gi, win in enumerate(POOL_WINDOWS):
        sl = slice(gi * gc, (gi + 1) * gc)
        acc = ext[:, sl]
        k = 1
        while k < win:
            acc = acc + pltpu.roll(acc, k, axis=0)
            k *= 2
        count = jnp.minimum(pos1, win).astype(jnp.float32)
        p = acc[POOL_HALO:, :] / count - hn[:, sl]
        y = jnp.dot(p.astype(jnp.bfloat16), w_ref[gi],
                    preferred_element_type=jnp.float32) + b_ref[:, sl]
        o_ref[:, sl] = h[:, sl] + y * sc_ref[:, sl]


def _pool_layer(h, g, w, b, scale):
    bsz, seq, d = h.shape
    tile = SEQ_TILE
    ng, gc, _ = w.shape
    return pl.pallas_call(
        functools.partial(_pool_kernel, tile=tile),
        grid=(bsz, seq // tile),
        in_specs=[
            _seq_tile_spec(tile, d),
            _resident((1, d)),
            _resident((ng, gc, gc)),
            _resident((1, d)),
            _resident((1, d)),
        ],
        out_specs=_seq_tile_spec(tile, d),
        out_shape=jax.ShapeDtypeStruct(h.shape, h.dtype),
        scratch_shapes=[pltpu.VMEM((POOL_HALO, d), jnp.float32)],
        compiler_params=_compiler_params(),
        name="pool_mixer",
    )(h, g.reshape(1, d), w.astype(jnp.bfloat16), b.reshape(1, d), scale.reshape(1, d))


def _ffn_kernel(h_ref, g_ref, win_ref, cw_ref, cb_ref, wout_ref, gf_ref, o_ref,
                carry_ref, *, tile, d_ff, final_norm):
    s = pl.program_id(1)

    @pl.when(s == 0)
    def _():
        carry_ref[...] = jnp.zeros_like(carry_ref)

    h = h_ref[...]
    hn = _rms_norm(h, g_ref[...]).astype(jnp.bfloat16)

    n_chunks = d_ff // FFN_CHUNK
    sub = FFN_SUB_TILE
    rows = sub // FFN_ROW_BLOCKS
    d = h.shape[-1]
    items = [(st, j) for st in range(tile // sub) for j in range(n_chunks)]

    def up_proj(i, branch):
        st, j = items[i]
        cs = slice(branch * d_ff + j * FFN_CHUNK, branch * d_ff + (j + 1) * FFN_CHUNK)
        u = jnp.dot(hn[st * sub:(st + 1) * sub, :], win_ref[:, cs],
                    preferred_element_type=jnp.float32)
        ext = jnp.concatenate([carry_ref[:, cs], u], axis=0)
        carry_ref[:, cs] = u[sub - SUBLANES:, :]
        return ext

    def causal_conv(ext, j, branch, rb):
        cs = slice(branch * d_ff + j * FFN_CHUNK, branch * d_ff + (j + 1) * FFN_CHUNK)
        blk = ext[rb * rows:(rb + 1) * rows + SUBLANES, :]
        u1 = pltpu.roll(blk, 1, axis=0)[SUBLANES:, :]
        u2 = pltpu.roll(blk, 2, axis=0)[SUBLANES:, :]
        c = blk[SUBLANES:, :] * cw_ref[2:3, cs] + u1 * cw_ref[1:2, cs] + u2 * cw_ref[0:1, cs]
        return c + cb_ref[:, cs]

    def gate(exts, i, rb):
        _, j = items[i]
        a = causal_conv(exts[0], j, 0, rb)
        g = causal_conv(exts[1], j, 1, rb)
        return (a * (g / (1.0 + jnp.exp(-g)))).astype(jnp.bfloat16)

    def down_proj(act, i, half):
        _, j = items[i]
        cols = slice(half * (d // 2), (half + 1) * (d // 2))
        return jnp.dot(act, wout_ref[j * FFN_CHUNK:(j + 1) * FFN_CHUNK, cols],
                       preferred_element_type=jnp.float32)

    acc = {}
    exts = {0: [up_proj(0, 0), up_proj(0, 1)]}
    acts = {}
    for i in range(len(items) + FFN_DOWN_DELAY):
        mxu_tasks = []
        if i + 1 < len(items):
            mxu_tasks += [("up", i + 1, 0), ("up", i + 1, 1)]
        if i >= FFN_DOWN_DELAY:
            mxu_tasks += [("down", i - FFN_DOWN_DELAY, 0), ("down", i - FFN_DOWN_DELAY, 1)]
        n_valu = FFN_ROW_BLOCKS if i < len(items) else 0
        blocks = []
        for step in range(max(n_valu, len(mxu_tasks))):
            if step < len(mxu_tasks):
                kind, it, arg = mxu_tasks[step]
                if kind == "up":
                    exts.setdefault(it, [None, None])[arg] = up_proj(it, arg)
                else:
                    st = items[it][0]
                    if (st, arg) not in acc:
                        cols = slice(arg * (d // 2), (arg + 1) * (d // 2))
                        acc[(st, arg)] = h[st * sub:(st + 1) * sub, cols]
                    acc[(st, arg)] = acc[(st, arg)] + down_proj(acts[it], it, arg)
            if step < n_valu:
                blocks.append(gate(exts[i], i, step))
        if blocks:
            acts[i] = jnp.concatenate(blocks, axis=0)
            del exts[i]
        acts.pop(i - FFN_DOWN_DELAY, None)
    for st in range(tile // sub):
        out = jnp.concatenate([acc[(st, 0)], acc[(st, 1)]], axis=1)
        if final_norm:
            out = _rms_norm(out, gf_ref[...])
        o_ref[st * sub:(st + 1) * sub, :] = out


def _ffn_layer(h, g, w_in, conv_w, conv_b, w_out, g_final=None):
    bsz, seq, d = h.shape
    tile = SEQ_TILE
    d_ff = w_out.shape[0]
    final_norm = g_final is not None
    gf = g_final if final_norm else jnp.ones((d,), h.dtype)
    return pl.pallas_call(
        functools.partial(_ffn_kernel, tile=tile, d_ff=d_ff, final_norm=final_norm),
        grid=(bsz, seq // tile),
        in_specs=[
            _seq_tile_spec(tile, d),
            _resident((1, d)),
            _resident((d, 2 * d_ff)),
            _resident((CONV_W, 2 * d_ff)),
            _resident((1, 2 * d_ff)),
            _resident((d_ff, d)),
            _resident((1, d)),
        ],
        out_specs=_seq_tile_spec(tile, d),
        out_shape=jax.ShapeDtypeStruct(h.shape, h.dtype),
        scratch_shapes=[pltpu.VMEM((SUBLANES, 2 * d_ff), jnp.float32)],
        compiler_params=_compiler_params(),
        name="conv_glu_ffn",
    )(h, g.reshape(1, d), w_in.astype(jnp.bfloat16), conv_w,
      conv_b.reshape(1, 2 * d_ff), w_out.astype(jnp.bfloat16), gf.reshape(1, d))


def _attn_kernel(sink_ref, h_ref, g_ref, wqkv_ref, bqkv_ref, wo_ref, o_ref,
                 kv_ref, att_ref, *, tile):
    s = pl.program_id(1)
    blk = ATTN_BLOCK
    nq = N_HEADS * HEAD_DIM
    nkv = N_KV_HEADS * HEAD_DIM

    @pl.when(s == 0)
    def _():
        kv_ref[0:blk, :] = jnp.zeros((blk, 2 * nkv), kv_ref.dtype)

    h = h_ref[...]
    hn = _rms_norm(h, g_ref[...]).astype(jnp.bfloat16)
    qkv = jnp.dot(hn, wqkv_ref[...], preferred_element_type=jnp.float32) + bqkv_ref[...]
    q = (qkv[:, :nq] * (HEAD_DIM ** -0.5)).astype(jnp.bfloat16)
    kv_ref[blk:, :] = qkv[:, nq:].astype(kv_ref.dtype)

    qi = lax.broadcasted_iota(jnp.int32, (blk, 2 * blk), 0)
    kj = lax.broadcasted_iota(jnp.int32, (blk, 2 * blk), 1)
    dist = qi + blk - kj
    band = (dist >= 0) & (dist < blk)
    dist_f = dist.astype(jnp.float32)

    for n in range(tile // blk):
        first_block = (s * (tile // blk) + n) == 0
        mask = band & ((kj >= blk) | jnp.logical_not(first_block))
        rows = slice(n * blk, (n + 1) * blk)
        for kh in range(N_KV_HEADS):
            kb = kv_ref[n * blk:(n + 2) * blk, kh * HEAD_DIM:(kh + 1) * HEAD_DIM]
            vb = kv_ref[n * blk:(n + 2) * blk, nkv + kh * HEAD_DIM:nkv + (kh + 1) * HEAD_DIM]
            for gq in range(N_GROUPS):
                hd = kh * N_GROUPS + gq
                cols = slice(hd * HEAD_DIM, (hd + 1) * HEAD_DIM)
                slope = math.pow(2.0, -8.0 / N_HEADS * (hd + 1))
                sc = lax.dot_general(q[rows, cols], kb, (((1,), (1,)), ((), ())),
                                     preferred_element_type=jnp.float32)
                sc = jnp.where(mask, sc - slope * dist_f, -jnp.inf)
                sink = sink_ref[hd]
                m = jnp.maximum(jnp.max(sc, axis=-1, keepdims=True), sink)
                p = jnp.exp(sc - m)
                denom = jnp.sum(p, axis=-1, keepdims=True) + jnp.exp(sink - m)
                pv = jnp.dot(p.astype(jnp.bfloat16), vb, preferred_element_type=jnp.float32)
                att_ref[rows, cols] = (pv / denom).astype(att_ref.dtype)

    kv_ref[0:blk, :] = kv_ref[tile:tile + blk, :]
    o_ref[...] = h + jnp.dot(att_ref[...], wo_ref[...], preferred_element_type=jnp.float32)


def _attn_layer(h, g, w_qkv, b_qkv, sinks, w_o):
    bsz, seq, d = h.shape
    tile = SEQ_TILE
    nq = N_HEADS * HEAD_DIM
    nkv = N_KV_HEADS * HEAD_DIM
    qkv_dim = nq + 2 * nkv
    return pl.pallas_call(
        functools.partial(_attn_kernel, tile=tile),
        grid=(bsz, seq // tile),
        in_specs=[
            pl.BlockSpec(memory_space=pltpu.SMEM),
            _seq_tile_spec(tile, d),
            _resident((1, d)),
            _resident((d, qkv_dim)),
            _resident((1, qkv_dim)),
            _resident((nq, d)),
        ],
        out_specs=_seq_tile_spec(tile, d),
        out_shape=jax.ShapeDtypeStruct(h.shape, h.dtype),
        scratch_shapes=[
            pltpu.VMEM((tile + ATTN_BLOCK, 2 * nkv), jnp.bfloat16),
            pltpu.VMEM((tile, nq), jnp.bfloat16),
        ],
        compiler_params=_compiler_params(),
        name="swa_sink_attention",
    )(sinks.astype(jnp.float32), h, g.reshape(1, d), w_qkv.astype(jnp.bfloat16),
      b_qkv.reshape(1, qkv_dim), w_o.astype(jnp.bfloat16))


def kernel(x, pool_w, pool_b, pool_scale, attn_w_qkv, attn_b_qkv, attn_sinks, attn_w_o,
           norm_mix, norm_ffn, ffn_w_in, ffn_conv_w, ffn_conv_b, ffn_w_out, norm_f):
    depth = norm_mix.shape[0]
    h = x
    for i in range(depth):
        j = i // 2
        if i % 2 == 0:
            h = _pool_layer(h, norm_mix[i], pool_w[j], pool_b[j], pool_scale[j])
        else:
            h = _attn_layer(h, norm_mix[i], attn_w_qkv[j], attn_b_qkv[j],
                            attn_sinks[j], attn_w_o[j])
        h = _ffn_layer(h, norm_ffn[i], ffn_w_in[i], ffn_conv_w[i], ffn_conv_b[i],
                       ffn_w_out[i], g_final=norm_f if i == depth - 1 else None)
    return h
```

```python
import functools
import math

import jax
import jax.numpy as jnp
from jax import lax
from jax.experimental import pallas as pl
from jax.experimental.pallas import tpu as pltpu

RMS_EPS = 1e-6
POOL_WINDOWS = (2, 4, 8, 16)
POOL_HALO = 16
HEAD_DIM = 64
N_HEADS = 16
N_KV_HEADS = 4
N_GROUPS = N_HEADS // N_KV_HEADS
ATTN_BLOCK = 128
CONV_W = 3
SUBLANES = 8

SEQ_TILE = 512
FFN_CHUNK = 256
FFN_SUB_TILE = 256
FFN_ROW_BLOCKS = 4
FFN_DOWN_DELAY = 2
ATTN_PV_DELAY = 2
VMEM_LIMIT_BYTES = 56 * 1024 * 1024


def _rms_norm(x, g):
    ms = jnp.mean(x * x, axis=-1, keepdims=True)
    return x * lax.rsqrt(ms + RMS_EPS) * g


def _compiler_params():
    return pltpu.CompilerParams(
        dimension_semantics=("arbitrary", "arbitrary"),
        vmem_limit_bytes=VMEM_LIMIT_BYTES,
    )


def _resident(shape):
    return pl.BlockSpec(shape, lambda b, s: (0,) * len(shape),
                        pipeline_mode=pl.Buffered(1))


def _seq_tile_spec(tile, d):
    return pl.BlockSpec((None, tile, d), lambda b, s: (b, s, 0))


def _pool_kernel(h_ref, g_ref, w_ref, b_ref, sc_ref, o_ref, halo_ref, *, tile):
    s = pl.program_id(1)

    @pl.when(s == 0)
    def _():
        halo_ref[...] = jnp.zeros_like(halo_ref)

    h = h_ref[...]
    hn = _rms_norm(h, g_ref[...])
    ext = jnp.concatenate([halo_ref[...], hn], axis=0)
    halo_ref[...] = hn[tile - POOL_HALO:, :]

    gc = w_ref.shape[1]
    pos1 = lax.broadcasted_iota(jnp.int32, (tile, gc), 0) + (s * tile + 1)
    for gi, win in enumerate(POOL_WINDOWS):
        sl = slice(gi * gc, (gi + 1) * gc)
        acc = ext[:, sl]
        k = 1
        while k < win:
            acc = acc + pltpu.roll(acc, k, axis=0)
            k *= 2
        count = jnp.minimum(pos1, win).astype(jnp.float32)
        p = acc[POOL_HALO:, :] / count - hn[:, sl]
        y = jnp.dot(p.astype(jnp.bfloat16), w_ref[gi],
                    preferred_element_type=jnp.float32) + b_ref[:, sl]
        o_ref[:, sl] = h[:, sl] + y * sc_ref[:, sl]


def _pool_layer(h, g, w, b, scale):
    bsz, seq, d = h.shape
    tile = SEQ_TILE
    ng, gc, _ = w.shape
    return pl.pallas_call(
        functools.partial(_pool_kernel, tile=tile),
        grid=(bsz, seq // tile),
        in_specs=[
            _seq_tile_spec(tile, d),
            _resident((1, d)),
            _resident((ng, gc, gc)),
            _resident((1, d)),
            _resident((1, d)),
        ],
        out_specs=_seq_tile_spec(tile, d),
        out_shape=jax.ShapeDtypeStruct(h.shape, h.dtype),
        scratch_shapes=[pltpu.VMEM((POOL_HALO, d), jnp.float32)],
        compiler_params=_compiler_params(),
        name="pool_mixer",
    )(h, g.reshape(1, d), w.astype(jnp.bfloat16), b.reshape(1, d), scale.reshape(1, d))


def _ffn_kernel(h_ref, g_ref, win_ref, cw_ref, cb_ref, wout_ref, gf_ref, o_ref,
                carry_ref, *, tile, d_ff, final_norm):
    s = pl.program_id(1)

    @pl.when(s == 0)
    def _():
        carry_ref[...] = jnp.zeros_like(carry_ref)

    h = h_ref[...]
    hn = _rms_norm(h, g_ref[...]).astype(jnp.bfloat16)

    n_chunks = d_ff // FFN_CHUNK
    sub = FFN_SUB_TILE
    rows = sub // FFN_ROW_BLOCKS
    d = h.shape[-1]
    items = [(st, j) for st in range(tile // sub) for j in range(n_chunks)]

    def up_proj(i, branch):
        st, j = items[i]
        cs = slice(branch * d_ff + j * FFN_CHUNK, branch * d_ff + (j + 1) * FFN_CHUNK)
        u = jnp.dot(hn[st * sub:(st + 1) * sub, :], win_ref[:, cs],
                    preferred_element_type=jnp.float32)
        ext = jnp.concatenate([carry_ref[:, cs], u], axis=0)
        carry_ref[:, cs] = u[sub - SUBLANES:, :]
        return ext

    def causal_conv(ext, j, branch, rb):
        cs = slice(branch * d_ff + j * FFN_CHUNK, branch * d_ff + (j + 1) * FFN_CHUNK)
        blk = ext[rb * rows:(rb + 1) * rows + SUBLANES, :]
        u1 = pltpu.roll(blk, 1, axis=0)[SUBLANES:, :]
        u2 = pltpu.roll(blk, 2, axis=0)[SUBLANES:, :]
        c = blk[SUBLANES:, :] * cw_ref[2:3, cs] + u1 * cw_ref[1:2, cs] + u2 * cw_ref[0:1, cs]
        return c + cb_ref[:, cs]

    def gate(exts, i, rb):
        _, j = items[i]
        a = causal_conv(exts[0], j, 0, rb)
        g = causal_conv(exts[1], j, 1, rb)
        return (a * (g / (1.0 + jnp.exp(-g)))).astype(jnp.bfloat16)

    def down_proj(act, i, half):
        _, j = items[i]
        cols = slice(half * (d // 2), (half + 1) * (d // 2))
        return jnp.dot(act, wout_ref[j * FFN_CHUNK:(j + 1) * FFN_CHUNK, cols],
                       preferred_element_type=jnp.float32)

    acc = {}
    exts = {0: [up_proj(0, 0), up_proj(0, 1)]}
    acts = {}
    for i in range(len(items) + FFN_DOWN_DELAY):
        mxu_tasks = []
        if i + 1 < len(items):
            mxu_tasks += [("up", i + 1, 0), ("up", i + 1, 1)]
        if i >= FFN_DOWN_DELAY:
            mxu_tasks += [("down", i - FFN_DOWN_DELAY, 0), ("down", i - FFN_DOWN_DELAY, 1)]
        n_valu = FFN_ROW_BLOCKS if i < len(items) else 0
        blocks = []
        for step in range(max(n_valu, len(mxu_tasks))):
            if step < len(mxu_tasks):
                kind, it, arg = mxu_tasks[step]
                if kind == "up":
                    exts.setdefault(it, [None, None])[arg] = up_proj(it, arg)
                else:
                    st = items[it][0]
                    if (st, arg) not in acc:
                        cols = slice(arg * (d // 2), (arg + 1) * (d // 2))
                        acc[(st, arg)] = h[st * sub:(st + 1) * sub, cols]
                    acc[(st, arg)] = acc[(st, arg)] + down_proj(acts[it], it, arg)
            if step < n_valu:
                blocks.append(gate(exts[i], i, step))
        if blocks:
            acts[i] = jnp.concatenate(blocks, axis=0)
            del exts[i]
        acts.pop(i - FFN_DOWN_DELAY, None)
    for st in range(tile // sub):
        out = jnp.concatenate([acc[(st, 0)], acc[(st, 1)]], axis=1)
        if final_norm:
            out = _rms_norm(out, gf_ref[...])
        o_ref[st * sub:(st + 1) * sub, :] = out


def _ffn_layer(h, g, w_in, conv_w, conv_b, w_out, g_final=None):
    bsz, seq, d = h.shape
    tile = SEQ_TILE
    d_ff = w_out.shape[0]
    final_norm = g_final is not None
    gf = g_final if final_norm else jnp.ones((d,), h.dtype)
    return pl.pallas_call(
        functools.partial(_ffn_kernel, tile=tile, d_ff=d_ff, final_norm=final_norm),
        grid=(bsz, seq // tile),
        in_specs=[
            _seq_tile_spec(tile, d),
            _resident((1, d)),
            _resident((d, 2 * d_ff)),
            _resident((CONV_W, 2 * d_ff)),
            _resident((1, 2 * d_ff)),
            _resident((d_ff, d)),
            _resident((1, d)),
        ],
        out_specs=_seq_tile_spec(tile, d),
        out_shape=jax.ShapeDtypeStruct(h.shape, h.dtype),
        scratch_shapes=[pltpu.VMEM((SUBLANES, 2 * d_ff), jnp.float32)],
        compiler_params=_compiler_params(),
        name="conv_glu_ffn",
    )(h, g.reshape(1, d), w_in.astype(jnp.bfloat16), conv_w,
      conv_b.reshape(1, 2 * d_ff), w_out.astype(jnp.bfloat16), gf.reshape(1, d))


def _attn_kernel(sink_ref, h_ref, g_ref, wqkv_ref, bqkv_ref, wo_ref, o_ref,
                 qm_ref, kd_ref, vd_ref, att_ref, bias_ref, *, tile):
    b = pl.program_id(0)
    s = pl.program_id(1)
    blk = ATTN_BLOCK
    lanes = 2 * HEAD_DIM
    nq = N_HEADS * HEAD_DIM
    nkv = N_KV_HEADS * HEAD_DIM
    nblk = tile // blk

    @pl.when(jnp.logical_and(b == 0, s == 0))
    def _():
        qi = lax.broadcasted_iota(jnp.int32, (blk, 2 * blk), 0)
        kj = lax.broadcasted_iota(jnp.int32, (blk, 2 * blk), 1)
        dist = qi + blk - kj
        band = (dist >= 0) & (dist < blk)
        dist_f = dist.astype(jnp.float32)
        for hd in range(N_HEADS):
            slope = math.pow(2.0, -8.0 / N_HEADS * (hd + 1))
            bias = jnp.where(band, -slope * dist_f, -jnp.inf)
            bias_ref[0, hd] = bias
            bias_ref[1, hd] = jnp.where(kj >= blk, bias, -jnp.inf)

    @pl.when(s == 0)
    def _():
        kd_ref[0:blk, :] = jnp.zeros((blk, kd_ref.shape[1]), kd_ref.dtype)
        vd_ref[0:blk, :] = jnp.zeros((blk, vd_ref.shape[1]), vd_ref.dtype)

    h = h_ref[...]
    hn = _rms_norm(h, g_ref[...]).astype(jnp.bfloat16)
    qkv = jnp.dot(hn, wqkv_ref[...], preferred_element_type=jnp.float32) + bqkv_ref[...]

    lo = lax.broadcasted_iota(jnp.int32, (tile, lanes), 1) < HEAD_DIM
    for pr in range(N_HEADS // 2):
        q2 = (qkv[:, pr * lanes:(pr + 1) * lanes] * (HEAD_DIM ** -0.5)).astype(jnp.bfloat16)
        zero = jnp.zeros_like(q2)
        qm_ref[2 * pr] = jnp.where(lo, q2, zero)
        qm_ref[2 * pr + 1] = jnp.where(lo, zero, q2)
    for src, dst in ((nq, kd_ref), (nq + nkv, vd_ref)):
        for pr in range(N_KV_HEADS // 2):
            x2 = qkv[:, src + pr * lanes:src + (pr + 1) * lanes].astype(jnp.bfloat16)
            r2 = pltpu.roll(x2, HEAD_DIM, axis=1)
            dst[blk:, (2 * pr) * lanes:(2 * pr + 1) * lanes] = jnp.where(lo, x2, r2)
            dst[blk:, (2 * pr + 1) * lanes:(2 * pr + 2) * lanes] = jnp.where(lo, r2, x2)

    ones = jnp.ones((2 * blk, lanes), jnp.bfloat16)
    lo_blk = lax.broadcasted_iota(jnp.int32, (blk, lanes), 1) < HEAD_DIM
    items = [(n, kh, pr) for n in range(nblk) for kh in range(N_KV_HEADS)
             for pr in range(N_GROUPS // 2)]

    def heads_of(i):
        _, kh, pr = items[i]
        return kh * N_GROUPS + 2 * pr, kh * N_GROUPS + 2 * pr + 1

    def scores(i):
        n, kh, _ = items[i]
        rows = slice(n * blk, (n + 1) * blk)
        q = jnp.concatenate([qm_ref[hd, rows, :] for hd in heads_of(i)], axis=0)
        k = kd_ref[n * blk:(n + 2) * blk, kh * lanes:(kh + 1) * lanes]
        return lax.dot_general(q, k, (((1,), (1,)), ((), ())),
                               preferred_element_type=jnp.float32)

    def softmax(sc, i):
        n = items[i][0]
        variant = jnp.where(s == 0, 1, 0) if n == 0 else 0
        ps, ms = [], []
        for t, hd in enumerate(heads_of(i)):
            x = sc[t * blk:(t + 1) * blk, :] + bias_ref[variant, hd]
            m = jnp.maximum(jnp.max(x, axis=-1, keepdims=True), sink_ref[hd])
            ps.append(jnp.exp(x - m).astype(jnp.bfloat16))
            ms.append(m)
        return jnp.concatenate(ps, axis=0), ms

    def weighted_values(p, i):
        n, kh, _ = items[i]
        v = vd_ref[n * blk:(n + 2) * blk, kh * lanes:(kh + 1) * lanes]
        rhs = jnp.concatenate([v, ones], axis=1)
        return jnp.dot(p, rhs, preferred_element_type=jnp.float32)

    def finish(pv, ms, i):
        n = items[i][0]
        outs = []
        for t, hd in enumerate(heads_of(i)):
            o = pv[t * blk:(t + 1) * blk, :lanes]
            denom = pv[t * blk:(t + 1) * blk, lanes:] + jnp.exp(sink_ref[hd] - ms[t])
            outs.append(o / denom)
        col = heads_of(i)[0] // 2
        att_ref[n * blk:(n + 1) * blk, col * lanes:(col + 1) * lanes] = (
            jnp.where(lo_blk, outs[0], outs[1]).astype(att_ref.dtype))

    sc = {0: scores(0)}
    pm = {}
    pv = {}
    n_items = len(items)
    for i in range(n_items + ATTN_PV_DELAY + 1):
        if i + 1 < n_items:
            sc[i + 1] = scores(i + 1)
        if 0 <= i - ATTN_PV_DELAY < n_items:
            j = i - ATTN_PV_DELAY
            pv[j] = weighted_values(pm[j][0], j)
        if i < n_items:
            pm[i] = softmax(sc.pop(i), i)
        if 0 <= i - ATTN_PV_DELAY - 1 < n_items:
            j = i - ATTN_PV_DELAY - 1
            finish(pv.pop(j), pm.pop(j)[1], j)

    kd_ref[0:blk, :] = kd_ref[tile:tile + blk, :]
    vd_ref[0:blk, :] = vd_ref[tile:tile + blk, :]
    o_ref[...] = h + jnp.dot(att_ref[...], wo_ref[...], preferred_element_type=jnp.float32)


def _attn_layer(h, g, w_qkv, b_qkv, sinks, w_o):
    bsz, seq, d = h.shape
    tile = SEQ_TILE
    nq = N_HEADS * HEAD_DIM
    nkv = N_KV_HEADS * HEAD_DIM
    qkv_dim = nq + 2 * nkv
    lanes = 2 * HEAD_DIM
    return pl.pallas_call(
        functools.partial(_attn_kernel, tile=tile),
        grid=(bsz, seq // tile),
        in_specs=[
            pl.BlockSpec(memory_space=pltpu.SMEM),
            _seq_tile_spec(tile, d),
            _resident((1, d)),
            _resident((d, qkv_dim)),
            _resident((1, qkv_dim)),
            _resident((nq, d)),
        ],
        out_specs=_seq_tile_spec(tile, d),
        out_shape=jax.ShapeDtypeStruct(h.shape, h.dtype),
        scratch_shapes=[
            pltpu.VMEM((N_HEADS, tile, lanes), jnp.bfloat16),
            pltpu.VMEM((tile + ATTN_BLOCK, N_KV_HEADS * lanes), jnp.bfloat16),
            pltpu.VMEM((tile + ATTN_BLOCK, N_KV_HEADS * lanes), jnp.bfloat16),
            pltpu.VMEM((tile, nq), jnp.bfloat16),
            pltpu.VMEM((2, N_HEADS, ATTN_BLOCK, 2 * ATTN_BLOCK), jnp.float32),
        ],
        compiler_params=_compiler_params(),
        name="swa_sink_attention",
    )(sinks.astype(jnp.float32), h, g.reshape(1, d), w_qkv.astype(jnp.bfloat16),
      b_qkv.reshape(1, qkv_dim), w_o.astype(jnp.bfloat16))


def kernel(x, pool_w, pool_b, pool_scale, attn_w_qkv, attn_b_qkv, attn_sinks, attn_w_o,
           norm_mix, norm_ffn, ffn_w_in, ffn_conv_w, ffn_conv_b, ffn_w_out, norm_f):
    depth = norm_mix.shape[0]
    h = x
    for i in range(depth):
        j = i // 2
        if i % 2 == 0:
            h = _pool_layer(h, norm_mix[i], pool_w[j], pool_b[j], pool_scale[j])
        else:
            h = _attn_layer(h, norm_mix[i], attn_w_qkv[j], attn_b_qkv[j],
                            attn_sinks[j], attn_w_o[j])
        h = _ffn_layer(h, norm_ffn[i], ffn_w_in[i], ffn_conv_w[i], ffn_conv_b[i],
                       ffn_w_out[i], g_final=norm_f if i == depth - 1 else None)
    return h
```

```python
import functools
import math

import jax
import jax.numpy as jnp
from jax import lax
from jax.experimental import pallas as pl
from jax.experimental.pallas import tpu as pltpu

RMS_EPS = 1e-6
POOL_WINDOWS = (2, 4, 8, 16)
POOL_HALO = 16
HEAD_DIM = 64
N_HEADS = 16
N_KV_HEADS = 4
N_GROUPS = N_HEADS // N_KV_HEADS
ATTN_BLOCK = 128
CONV_W = 3
SUBLANES = 8

SEQ_TILE = 1024
FFN_CHUNK = 256
FFN_SUB_TILE = 256
FFN_ROW_BLOCKS = 4
FFN_DOWN_DELAY = 2
ATTN_PV_DELAY = 2
VMEM_LIMIT_BYTES = 56 * 1024 * 1024


def _rms_norm(x, g):
    ms = jnp.mean(x * x, axis=-1, keepdims=True)
    return x * lax.rsqrt(ms + RMS_EPS) * g


def _compiler_params():
    return pltpu.CompilerParams(
        dimension_semantics=("arbitrary", "arbitrary"),
        vmem_limit_bytes=VMEM_LIMIT_BYTES,
    )


def _resident(shape, layer=None):
    if layer is None:
        return pl.BlockSpec(shape, lambda b, s: (0,) * len(shape),
                            pipeline_mode=pl.Buffered(1))
    return pl.BlockSpec((None,) + tuple(shape), lambda b, s: (layer,) + (0,) * len(shape),
                        pipeline_mode=pl.Buffered(1))


def _seq_tile_spec(tile, d):
    return pl.BlockSpec((None, tile, d), lambda b, s: (b, s, 0))


def _pool_kernel(h_ref, g_ref, w_ref, b_ref, sc_ref, o_ref, halo_ref, *, tile):
    s = pl.program_id(1)

    @pl.when(s == 0)
    def _():
        halo_ref[...] = jnp.zeros_like(halo_ref)

    h = h_ref[...]
    hn = _rms_norm(h, g_ref[...])
    ext = jnp.concatenate([halo_ref[...], hn], axis=0)
    halo_ref[...] = hn[tile - POOL_HALO:, :]

    gc = w_ref.shape[1]
    pos1 = lax.broadcasted_iota(jnp.int32, (tile, gc), 0) + (s * tile + 1)
    for gi, win in enumerate(POOL_WINDOWS):
        sl = slice(gi * gc, (gi + 1) * gc)
        acc = ext[:, sl]
        k = 1
        while k < win:
            acc = acc + pltpu.roll(acc, k, axis=0)
            k *= 2
        count = jnp.minimum(pos1, win).astype(jnp.float32)
        p = acc[POOL_HALO:, :] / count - hn[:, sl]
        y = jnp.dot(p.astype(jnp.bfloat16), w_ref[gi],
                    preferred_element_type=jnp.float32) + b_ref[:, sl]
        o_ref[:, sl] = h[:, sl] + y * sc_ref[:, sl]


def _pool_layer(h, layer, j, g, w, b, scale):
    bsz, seq, d = h.shape
    tile = SEQ_TILE
    _, ng, gc, _ = w.shape
    return pl.pallas_call(
        functools.partial(_pool_kernel, tile=tile),
        grid=(bsz, seq // tile),
        in_specs=[
            _seq_tile_spec(tile, d),
            _resident((1, d), layer),
            _resident((ng, gc, gc), j),
            _resident((1, d), j),
            _resident((1, d), j),
        ],
        out_specs=_seq_tile_spec(tile, d),
        out_shape=jax.ShapeDtypeStruct(h.shape, h.dtype),
        scratch_shapes=[pltpu.VMEM((POOL_HALO, d), jnp.float32)],
        compiler_params=_compiler_params(),
        name="pool_mixer",
    )(h, g[:, None, :], w, b.reshape(-1, 1, d), scale[:, None, :])


def _ffn_kernel(h_ref, g_ref, win_ref, cw_ref, cb_ref, wout_ref, gf_ref, o_ref,
                carry_ref, *, tile, d_ff, final_norm):
    s = pl.program_id(1)

    @pl.when(s == 0)
    def _():
        carry_ref[...] = jnp.zeros_like(carry_ref)

    h = h_ref[...]
    hn = _rms_norm(h, g_ref[...]).astype(jnp.bfloat16)

    n_chunks = d_ff // FFN_CHUNK
    sub = FFN_SUB_TILE
    rows = sub // FFN_ROW_BLOCKS
    d = h.shape[-1]
    items = [(st, j) for st in range(tile // sub) for j in range(n_chunks)]

    def up_proj(i, branch):
        st, j = items[i]
        cs = slice(branch * d_ff + j * FFN_CHUNK, branch * d_ff + (j + 1) * FFN_CHUNK)
        u = jnp.dot(hn[st * sub:(st + 1) * sub, :], win_ref[:, cs],
                    preferred_element_type=jnp.float32)
        ext = jnp.concatenate([carry_ref[:, cs], u], axis=0)
        carry_ref[:, cs] = u[sub - SUBLANES:, :]
        return ext

    def causal_conv(ext, j, branch, rb):
        cs = slice(branch * d_ff + j * FFN_CHUNK, branch * d_ff + (j + 1) * FFN_CHUNK)
        blk = ext[rb * rows:(rb + 1) * rows + SUBLANES, :]
        u1 = pltpu.roll(blk, 1, axis=0)[SUBLANES:, :]
        u2 = pltpu.roll(blk, 2, axis=0)[SUBLANES:, :]
        c = blk[SUBLANES:, :] * cw_ref[2:3, cs] + u1 * cw_ref[1:2, cs] + u2 * cw_ref[0:1, cs]
        return c + cb_ref[:, cs]

    def gate(exts, i, rb):
        _, j = items[i]
        a = causal_conv(exts[0], j, 0, rb)
        g = causal_conv(exts[1], j, 1, rb)
        return (a * (g / (1.0 + jnp.exp(-g)))).astype(jnp.bfloat16)

    def down_proj(act, i, half):
        _, j = items[i]
        cols = slice(half * (d // 2), (half + 1) * (d // 2))
        return jnp.dot(act, wout_ref[j * FFN_CHUNK:(j + 1) * FFN_CHUNK, cols],
                       preferred_element_type=jnp.float32)

    acc = {}
    exts = {0: [up_proj(0, 0), up_proj(0, 1)]}
    acts = {}
    for i in range(len(items) + FFN_DOWN_DELAY):
        mxu_tasks = []
        if i + 1 < len(items):
            mxu_tasks += [("up", i + 1, 0), ("up", i + 1, 1)]
        if i >= FFN_DOWN_DELAY:
            mxu_tasks += [("down", i - FFN_DOWN_DELAY, 0), ("down", i - FFN_DOWN_DELAY, 1)]
        n_valu = FFN_ROW_BLOCKS if i < len(items) else 0
        blocks = []
        for step in range(max(n_valu, len(mxu_tasks))):
            if step < len(mxu_tasks):
                kind, it, arg = mxu_tasks[step]
                if kind == "up":
                    exts.setdefault(it, [None, None])[arg] = up_proj(it, arg)
                else:
                    st = items[it][0]
                    if (st, arg) not in acc:
                        cols = slice(arg * (d // 2), (arg + 1) * (d // 2))
                        acc[(st, arg)] = h[st * sub:(st + 1) * sub, cols]
                    acc[(st, arg)] = acc[(st, arg)] + down_proj(acts[it], it, arg)
            if step < n_valu:
                blocks.append(gate(exts[i], i, step))
        if blocks:
            acts[i] = jnp.concatenate(blocks, axis=0)
            del exts[i]
        acts.pop(i - FFN_DOWN_DELAY, None)
    for st in range(tile // sub):
        out = jnp.concatenate([acc[(st, 0)], acc[(st, 1)]], axis=1)
        if final_norm:
            out = _rms_norm(out, gf_ref[...])
        o_ref[st * sub:(st + 1) * sub, :] = out


def _ffn_layer(h, layer, g, w_in, conv_w, conv_b, w_out, g_final=None):
    bsz, seq, d = h.shape
    tile = SEQ_TILE
    d_ff = w_out.shape[1]
    final_norm = g_final is not None
    gf = g_final if final_norm else jnp.ones((d,), h.dtype)
    return pl.pallas_call(
        functools.partial(_ffn_kernel, tile=tile, d_ff=d_ff, final_norm=final_norm),
        grid=(bsz, seq // tile),
        in_specs=[
            _seq_tile_spec(tile, d),
            _resident((1, d), layer),
            _resident((d, 2 * d_ff), layer),
            _resident((CONV_W, 2 * d_ff), layer),
            _resident((1, 2 * d_ff), layer),
            _resident((d_ff, d), layer),
            _resident((1, d)),
        ],
        out_specs=_seq_tile_spec(tile, d),
        out_shape=jax.ShapeDtypeStruct(h.shape, h.dtype),
        scratch_shapes=[pltpu.VMEM((SUBLANES, 2 * d_ff), jnp.float32)],
        compiler_params=_compiler_params(),
        name="conv_glu_ffn",
    )(h, g[:, None, :], w_in, conv_w, conv_b[:, None, :], w_out, gf.reshape(1, d))


def _attn_kernel(sink_ref, h_ref, g_ref, wqkv_ref, bqkv_ref, wo_ref, o_ref,
                 qm_ref, kd_ref, vd_ref, att_ref, bias_ref, *, tile):
    b = pl.program_id(0)
    s = pl.program_id(1)
    blk = ATTN_BLOCK
    lanes = 2 * HEAD_DIM
    nq = N_HEADS * HEAD_DIM
    nkv = N_KV_HEADS * HEAD_DIM
    nblk = tile // blk

    @pl.when(jnp.logical_and(b == 0, s == 0))
    def _():
        qi = lax.broadcasted_iota(jnp.int32, (blk, 2 * blk), 0)
        kj = lax.broadcasted_iota(jnp.int32, (blk, 2 * blk), 1)
        dist = qi + blk - kj
        band = (dist >= 0) & (dist < blk)
        dist_f = dist.astype(jnp.float32)
        for hd in range(N_HEADS):
            slope = math.pow(2.0, -8.0 / N_HEADS * (hd + 1))
            bias = jnp.where(band, -slope * dist_f, -jnp.inf)
            bias_ref[0, hd] = bias
            bias_ref[1, hd] = jnp.where(kj >= blk, bias, -jnp.inf)

    @pl.when(s == 0)
    def _():
        kd_ref[0:blk, :] = jnp.zeros((blk, kd_ref.shape[1]), kd_ref.dtype)
        vd_ref[0:blk, :] = jnp.zeros((blk, vd_ref.shape[1]), vd_ref.dtype)

    h = h_ref[...]
    hn = _rms_norm(h, g_ref[...]).astype(jnp.bfloat16)
    qkv = jnp.dot(hn, wqkv_ref[...], preferred_element_type=jnp.float32) + bqkv_ref[...]

    lo = lax.broadcasted_iota(jnp.int32, (tile, lanes), 1) < HEAD_DIM
    for pr in range(N_HEADS // 2):
        q2 = (qkv[:, pr * lanes:(pr + 1) * lanes] * (HEAD_DIM ** -0.5)).astype(jnp.bfloat16)
        zero = jnp.zeros_like(q2)
        qm_ref[2 * pr] = jnp.where(lo, q2, zero)
        qm_ref[2 * pr + 1] = jnp.where(lo, zero, q2)
    for src, dst in ((nq, kd_ref), (nq + nkv, vd_ref)):
        for pr in range(N_KV_HEADS // 2):
            x2 = qkv[:, src + pr * lanes:src + (pr + 1) * lanes].astype(jnp.bfloat16)
            r2 = pltpu.roll(x2, HEAD_DIM, axis=1)
            dst[blk:, (2 * pr) * lanes:(2 * pr + 1) * lanes] = jnp.where(lo, x2, r2)
            dst[blk:, (2 * pr + 1) * lanes:(2 * pr + 2) * lanes] = jnp.where(lo, r2, x2)

    ones = jnp.ones((2 * blk, lanes), jnp.bfloat16)
    lo_blk = lax.broadcasted_iota(jnp.int32, (blk, lanes), 1) < HEAD_DIM
    items = [(n, kh, pr) for n in range(nblk) for kh in range(N_KV_HEADS)
             for pr in range(N_GROUPS // 2)]

    def heads_of(i):
        _, kh, pr = items[i]
        return kh * N_GROUPS + 2 * pr, kh * N_GROUPS + 2 * pr + 1

    def scores(i):
        n, kh, _ = items[i]
        rows = slice(n * blk, (n + 1) * blk)
        q = jnp.concatenate([qm_ref[hd, rows, :] for hd in heads_of(i)], axis=0)
        k = kd_ref[n * blk:(n + 2) * blk, kh * lanes:(kh + 1) * lanes]
        return lax.dot_general(q, k, (((1,), (1,)), ((), ())),
                               preferred_element_type=jnp.float32)

    def softmax(sc, i):
        n = items[i][0]
        variant = jnp.where(s == 0, 1, 0) if n == 0 else 0
        ps, ms = [], []
        for t, hd in enumerate(heads_of(i)):
            x = sc[t * blk:(t + 1) * blk, :] + bias_ref[variant, hd]
            m = jnp.maximum(jnp.max(x, axis=-1, keepdims=True), sink_ref[hd])
            ps.append(jnp.exp(x - m).astype(jnp.bfloat16))
            ms.append(m)
        return jnp.concatenate(ps, axis=0), ms

    def weighted_values(p, i):
        n, kh, _ = items[i]
        v = vd_ref[n * blk:(n + 2) * blk, kh * lanes:(kh + 1) * lanes]
        rhs = jnp.concatenate([v, ones], axis=1)
        return jnp.dot(p, rhs, preferred_element_type=jnp.float32)

    def finish(pv, ms, i):
        n = items[i][0]
        outs = []
        for t, hd in enumerate(heads_of(i)):
            o = pv[t * blk:(t + 1) * blk, :lanes]
            denom = pv[t * blk:(t + 1) * blk, lanes:] + jnp.exp(sink_ref[hd] - ms[t])
            outs.append(o / denom)
        col = heads_of(i)[0] // 2
        att_ref[n * blk:(n + 1) * blk, col * lanes:(col + 1) * lanes] = (
            jnp.where(lo_blk, outs[0], outs[1]).astype(att_ref.dtype))

    sc = {0: scores(0)}
    pm = {}
    pv = {}
    n_items = len(items)
    for i in range(n_items + ATTN_PV_DELAY + 1):
        if i + 1 < n_items:
            sc[i + 1] = scores(i + 1)
        if 0 <= i - ATTN_PV_DELAY < n_items:
            j = i - ATTN_PV_DELAY
            pv[j] = weighted_values(pm[j][0], j)
        if i < n_items:
            pm[i] = softmax(sc.pop(i), i)
        if 0 <= i - ATTN_PV_DELAY - 1 < n_items:
            j = i - ATTN_PV_DELAY - 1
            finish(pv.pop(j), pm.pop(j)[1], j)

    kd_ref[0:blk, :] = kd_ref[tile:tile + blk, :]
    vd_ref[0:blk, :] = vd_ref[tile:tile + blk, :]
    o_ref[...] = h + jnp.dot(att_ref[...], wo_ref[...], preferred_element_type=jnp.float32)


def _attn_layer(h, layer, j, g, w_qkv, b_qkv, sinks, w_o):
    bsz, seq, d = h.shape
    tile = SEQ_TILE
    nq = N_HEADS * HEAD_DIM
    nkv = N_KV_HEADS * HEAD_DIM
    qkv_dim = nq + 2 * nkv
    lanes = 2 * HEAD_DIM
    return pl.pallas_call(
        functools.partial(_attn_kernel, tile=tile),
        grid=(bsz, seq // tile),
        in_specs=[
            pl.BlockSpec(memory_space=pltpu.SMEM),
            _seq_tile_spec(tile, d),
            _resident((1, d), layer),
            _resident((d, qkv_dim), j),
            _resident((1, qkv_dim), j),
            _resident((nq, d), j),
        ],
        out_specs=_seq_tile_spec(tile, d),
        out_shape=jax.ShapeDtypeStruct(h.shape, h.dtype),
        scratch_shapes=[
            pltpu.VMEM((N_HEADS, tile, lanes), jnp.bfloat16),
            pltpu.VMEM((tile + ATTN_BLOCK, N_KV_HEADS * lanes), jnp.bfloat16),
            pltpu.VMEM((tile + ATTN_BLOCK, N_KV_HEADS * lanes), jnp.bfloat16),
            pltpu.VMEM((tile, nq), jnp.bfloat16),
            pltpu.VMEM((2, N_HEADS, ATTN_BLOCK, 2 * ATTN_BLOCK), jnp.float32),
        ],
        compiler_params=_compiler_params(),
        name="swa_sink_attention",
    )(sinks[j], h, g[:, None, :], w_qkv, b_qkv[:, None, :], w_o)


def kernel(x, pool_w, pool_b, pool_scale, attn_w_qkv, attn_b_qkv, attn_sinks, attn_w_o,
           norm_mix, norm_ffn, ffn_w_in, ffn_conv_w, ffn_conv_b, ffn_w_out, norm_f):
    depth = norm_mix.shape[0]
    w_in_bf16 = ffn_w_in.astype(jnp.bfloat16)
    w_out_bf16 = ffn_w_out.astype(jnp.bfloat16)
    pool_w_bf16 = pool_w.astype(jnp.bfloat16)
    w_qkv_bf16 = attn_w_qkv.astype(jnp.bfloat16)
    w_o_bf16 = attn_w_o.astype(jnp.bfloat16)
    h = x
    for i in range(depth):
        j = i // 2
        if i % 2 == 0:
            h = _pool_layer(h, i, j, norm_mix, pool_w_bf16, pool_b, pool_scale)
        else:
            h = _attn_layer(h, i, j, norm_mix, w_qkv_bf16, attn_b_qkv, attn_sinks, w_o_bf16)
        h = _ffn_layer(h, i, norm_ffn, w_in_bf16, ffn_conv_w, ffn_conv_b, w_out_bf16,
                       g_final=norm_f if i == depth - 1 else None)
    return h
```

```python
import functools
import math

import jax
import jax.numpy as jnp
from jax import lax
from jax.experimental import pallas as pl
from jax.experimental.pallas import tpu as pltpu

RMS_EPS = 1e-6
POOL_WINDOWS = (2, 4, 8, 16)
POOL_HALO = 16
HEAD_DIM = 64
N_HEADS = 16
N_KV_HEADS = 4
N_GROUPS = N_HEADS // N_KV_HEADS
ATTN_BLOCK = 128
CONV_W = 3
SUBLANES = 8

SEQ_TILE = 1024
FFN_SEQ_TILE = 512
FFN_CHUNK = 256
FFN_SUB_TILE = 256
FFN_ROW_BLOCKS = 4
FFN_DOWN_DELAY = 2
ATTN_PV_DELAY = 2
VMEM_LIMIT_BYTES = 56 * 1024 * 1024


def _rms_norm(x, g):
    ms = jnp.mean(x * x, axis=-1, keepdims=True)
    return x * lax.rsqrt(ms + RMS_EPS) * g


def _compiler_params():
    return pltpu.CompilerParams(
        dimension_semantics=("arbitrary", "arbitrary"),
        vmem_limit_bytes=VMEM_LIMIT_BYTES,
    )


def _resident(shape, layer=None):
    if layer is None:
        return pl.BlockSpec(shape, lambda b, s: (0,) * len(shape),
                            pipeline_mode=pl.Buffered(1))
    return pl.BlockSpec((None,) + tuple(shape), lambda b, s: (layer,) + (0,) * len(shape),
                        pipeline_mode=pl.Buffered(1))


def _seq_tile_spec(tile, d):
    return pl.BlockSpec((None, tile, d), lambda b, s: (b, s, 0))


def _pool_kernel(h_ref, g_ref, w_ref, b_ref, sc_ref, o_ref, halo_ref, *, tile):
    s = pl.program_id(1)

    @pl.when(s == 0)
    def _():
        halo_ref[...] = jnp.zeros_like(halo_ref)

    h = h_ref[...]
    hn = _rms_norm(h, g_ref[...])
    ext = jnp.concatenate([halo_ref[...], hn], axis=0)
    halo_ref[...] = hn[tile - POOL_HALO:, :]

    gc = w_ref.shape[1]
    lanes = 128
    pos1 = lax.broadcasted_iota(jnp.int32, (tile, lanes), 0) + (s * tile + 1)
    inv_pos = 1.0 / pos1.astype(jnp.float32)
    for gi, win in enumerate(POOL_WINDOWS):
        sl = slice(gi * gc, (gi + 1) * gc)
        acc = ext[:, sl]
        k = 1
        while k < win:
            acc = acc + pltpu.roll(acc, k, axis=0)
            k *= 2
        inv_count = jnp.where(pos1 < win, inv_pos, 1.0 / win)
        inv_count = jnp.concatenate([inv_count] * (gc // lanes), axis=1)
        p = acc[POOL_HALO:, :] * inv_count - hn[:, sl]
        y = jnp.dot(p.astype(jnp.bfloat16), w_ref[gi],
                    preferred_element_type=jnp.float32) + b_ref[:, sl]
        o_ref[:, sl] = h[:, sl] + y * sc_ref[:, sl]


def _pool_layer(h, layer, j, g, w, b, scale):
    bsz, seq, d = h.shape
    tile = SEQ_TILE
    _, ng, gc, _ = w.shape
    return pl.pallas_call(
        functools.partial(_pool_kernel, tile=tile),
        grid=(bsz, seq // tile),
        in_specs=[
            _seq_tile_spec(tile, d),
            _resident((1, d), layer),
            _resident((ng, gc, gc), j),
            _resident((1, d), j),
            _resident((1, d), j),
        ],
        out_specs=_seq_tile_spec(tile, d),
        out_shape=jax.ShapeDtypeStruct(h.shape, h.dtype),
        scratch_shapes=[pltpu.VMEM((POOL_HALO, d), jnp.float32)],
        compiler_params=_compiler_params(),
        name="pool_mixer",
    )(h, g[:, None, :], w, b.reshape(-1, 1, d), scale[:, None, :])


def _ffn_kernel(h_ref, g_ref, win_ref, cw_ref, cb_ref, wout_ref, gf_ref, o_ref,
                carry_ref, *, tile, d_ff, final_norm):
    s = pl.program_id(1)

    @pl.when(s == 0)
    def _():
        carry_ref[...] = jnp.zeros_like(carry_ref)

    h = h_ref[...]
    hn = _rms_norm(h, g_ref[...]).astype(jnp.bfloat16)

    n_chunks = d_ff // FFN_CHUNK
    sub = FFN_SUB_TILE
    rows = sub // FFN_ROW_BLOCKS
    d = h.shape[-1]
    items = [(st, j) for st in range(tile // sub) for j in range(n_chunks)]

    def up_proj(i, branch):
        st, j = items[i]
        cs = slice(branch * d_ff + j * FFN_CHUNK, branch * d_ff + (j + 1) * FFN_CHUNK)
        u = jnp.dot(hn[st * sub:(st + 1) * sub, :], win_ref[:, cs],
                    preferred_element_type=jnp.float32)
        ext = jnp.concatenate([carry_ref[:, cs], u], axis=0)
        carry_ref[:, cs] = u[sub - SUBLANES:, :]
        return ext

    def causal_conv(ext, j, branch, rb):
        cs = slice(branch * d_ff + j * FFN_CHUNK, branch * d_ff + (j + 1) * FFN_CHUNK)
        blk = ext[rb * rows:(rb + 1) * rows + SUBLANES, :]
        u1 = pltpu.roll(blk, 1, axis=0)[SUBLANES:, :]
        u2 = pltpu.roll(blk, 2, axis=0)[SUBLANES:, :]
        c = blk[SUBLANES:, :] * cw_ref[2:3, cs] + u1 * cw_ref[1:2, cs] + u2 * cw_ref[0:1, cs]
        return c + cb_ref[:, cs]

    def gate(exts, i, rb):
        _, j = items[i]
        a = causal_conv(exts[0], j, 0, rb)
        g = causal_conv(exts[1], j, 1, rb)
        return (a * (g / (1.0 + jnp.exp(-g)))).astype(jnp.bfloat16)

    def down_proj(act, i, half):
        _, j = items[i]
        cols = slice(half * (d // 2), (half + 1) * (d // 2))
        return jnp.dot(act, wout_ref[j * FFN_CHUNK:(j + 1) * FFN_CHUNK, cols],
                       preferred_element_type=jnp.float32)

    acc = {}
    exts = {0: [up_proj(0, 0), up_proj(0, 1)]}
    acts = {}
    for i in range(len(items) + FFN_DOWN_DELAY):
        mxu_tasks = []
        if i + 1 < len(items):
            mxu_tasks += [("up", i + 1, 0), ("up", i + 1, 1)]
        if i >= FFN_DOWN_DELAY:
            mxu_tasks += [("down", i - FFN_DOWN_DELAY, 0), ("down", i - FFN_DOWN_DELAY, 1)]
        n_valu = FFN_ROW_BLOCKS if i < len(items) else 0
        blocks = []
        for step in range(max(n_valu, len(mxu_tasks))):
            if step < len(mxu_tasks):
                kind, it, arg = mxu_tasks[step]
                if kind == "up":
                    exts.setdefault(it, [None, None])[arg] = up_proj(it, arg)
                else:
                    st = items[it][0]
                    if (st, arg) not in acc:
                        cols = slice(arg * (d // 2), (arg + 1) * (d // 2))
                        acc[(st, arg)] = h[st * sub:(st + 1) * sub, cols]
                    acc[(st, arg)] = acc[(st, arg)] + down_proj(acts[it], it, arg)
            if step < n_valu:
                blocks.append(gate(exts[i], i, step))
        if blocks:
            acts[i] = jnp.concatenate(blocks, axis=0)
            del exts[i]
        acts.pop(i - FFN_DOWN_DELAY, None)
    for st in range(tile // sub):
        out = jnp.concatenate([acc[(st, 0)], acc[(st, 1)]], axis=1)
        if final_norm:
            out = _rms_norm(out, gf_ref[...])
        o_ref[st * sub:(st + 1) * sub, :] = out


def _ffn_layer(h, layer, g, w_in, conv_w, conv_b, w_out, g_final=None):
    bsz, seq, d = h.shape
    tile = FFN_SEQ_TILE
    d_ff = w_out.shape[1]
    final_norm = g_final is not None
    gf = g_final if final_norm else jnp.ones((d,), h.dtype)
    return pl.pallas_call(
        functools.partial(_ffn_kernel, tile=tile, d_ff=d_ff, final_norm=final_norm),
        grid=(bsz, seq // tile),
        in_specs=[
            _seq_tile_spec(tile, d),
            _resident((1, d), layer),
            _resident((d, 2 * d_ff), layer),
            _resident((CONV_W, 2 * d_ff), layer),
            _resident((1, 2 * d_ff), layer),
            _resident((d_ff, d), layer),
            _resident((1, d)),
        ],
        out_specs=_seq_tile_spec(tile, d),
        out_shape=jax.ShapeDtypeStruct(h.shape, h.dtype),
        scratch_shapes=[pltpu.VMEM((SUBLANES, 2 * d_ff), jnp.float32)],
        compiler_params=_compiler_params(),
        name="conv_glu_ffn",
    )(h, g[:, None, :], w_in, conv_w, conv_b[:, None, :], w_out, gf.reshape(1, d))


def _attn_kernel(sink_ref, h_ref, g_ref, wqkv_ref, bqkv_ref, wo_ref, o_ref,
                 qm_ref, kd_ref, vd_ref, att_ref, bias_ref, *, tile):
    b = pl.program_id(0)
    s = pl.program_id(1)
    blk = ATTN_BLOCK
    lanes = 2 * HEAD_DIM
    nq = N_HEADS * HEAD_DIM
    nkv = N_KV_HEADS * HEAD_DIM
    nblk = tile // blk

    @pl.when(jnp.logical_and(b == 0, s == 0))
    def _():
        qi = lax.broadcasted_iota(jnp.int32, (blk, 2 * blk), 0)
        kj = lax.broadcasted_iota(jnp.int32, (blk, 2 * blk), 1)
        dist = qi + blk - kj
        band = (dist >= 0) & (dist < blk)
        dist_f = dist.astype(jnp.float32)
        for hd in range(N_HEADS):
            slope = math.pow(2.0, -8.0 / N_HEADS * (hd + 1))
            bias = jnp.where(band, -slope * dist_f, -jnp.inf)
            bias_ref[0, hd] = bias
            bias_ref[1, hd] = jnp.where(kj >= blk, bias, -jnp.inf)

    @pl.when(s == 0)
    def _():
        kd_ref[0:blk, :] = jnp.zeros((blk, kd_ref.shape[1]), kd_ref.dtype)
        vd_ref[0:blk, :] = jnp.zeros((blk, vd_ref.shape[1]), vd_ref.dtype)

    h = h_ref[...]
    hn = _rms_norm(h, g_ref[...]).astype(jnp.bfloat16)
    qkv = jnp.dot(hn, wqkv_ref[...], preferred_element_type=jnp.float32) + bqkv_ref[...]

    lo = lax.broadcasted_iota(jnp.int32, (tile, lanes), 1) < HEAD_DIM
    for pr in range(N_HEADS // 2):
        q2 = (qkv[:, pr * lanes:(pr + 1) * lanes] * (HEAD_DIM ** -0.5)).astype(jnp.bfloat16)
        zero = jnp.zeros_like(q2)
        qm_ref[2 * pr] = jnp.where(lo, q2, zero)
        qm_ref[2 * pr + 1] = jnp.where(lo, zero, q2)
    for src, dst in ((nq, kd_ref), (nq + nkv, vd_ref)):
        for pr in range(N_KV_HEADS // 2):
            x2 = qkv[:, src + pr * lanes:src + (pr + 1) * lanes].astype(jnp.bfloat16)
            r2 = pltpu.roll(x2, HEAD_DIM, axis=1)
            dst[blk:, (2 * pr) * lanes:(2 * pr + 1) * lanes] = jnp.where(lo, x2, r2)
            dst[blk:, (2 * pr + 1) * lanes:(2 * pr + 2) * lanes] = jnp.where(lo, r2, x2)

    ones = jnp.ones((2 * blk, lanes), jnp.bfloat16)
    lo_blk = lax.broadcasted_iota(jnp.int32, (blk, lanes), 1) < HEAD_DIM
    items = [(n, kh, pr) for n in range(nblk) for kh in range(N_KV_HEADS)
             for pr in range(N_GROUPS // 2)]

    def heads_of(i):
        _, kh, pr = items[i]
        return kh * N_GROUPS + 2 * pr, kh * N_GROUPS + 2 * pr + 1

    def scores(i):
        n, kh, _ = items[i]
        rows = slice(n * blk, (n + 1) * blk)
        q = jnp.concatenate([qm_ref[hd, rows, :] for hd in heads_of(i)], axis=0)
        k = kd_ref[n * blk:(n + 2) * blk, kh * lanes:(kh + 1) * lanes]
        return lax.dot_general(q, k, (((1,), (1,)), ((), ())),
                               preferred_element_type=jnp.float32)

    def softmax(sc, i):
        n = items[i][0]
        variant = jnp.where(s == 0, 1, 0) if n == 0 else 0
        ps, ms = [], []
        for t, hd in enumerate(heads_of(i)):
            x = sc[t * blk:(t + 1) * blk, :] + bias_ref[variant, hd]
            m = jnp.maximum(jnp.max(x, axis=-1, keepdims=True), sink_ref[hd])
            ps.append(jnp.exp(x - m).astype(jnp.bfloat16))
            ms.append(m)
        return jnp.concatenate(ps, axis=0), ms

    def weighted_values(p, i):
        n, kh, _ = items[i]
        v = vd_ref[n * blk:(n + 2) * blk, kh * lanes:(kh + 1) * lanes]
        rhs = jnp.concatenate([v, ones], axis=1)
        return jnp.dot(p, rhs, preferred_element_type=jnp.float32)

    def finish(pv, ms, i):
        n = items[i][0]
        outs = []
        for t, hd in enumerate(heads_of(i)):
            o = pv[t * blk:(t + 1) * blk, :lanes]
            denom = pv[t * blk:(t + 1) * blk, lanes:] + jnp.exp(sink_ref[hd] - ms[t])
            outs.append(o / denom)
        col = heads_of(i)[0] // 2
        att_ref[n * blk:(n + 1) * blk, col * lanes:(col + 1) * lanes] = (
            jnp.where(lo_blk, outs[0], outs[1]).astype(att_ref.dtype))

    sc = {0: scores(0)}
    pm = {}
    pv = {}
    n_items = len(items)
    for i in range(n_items + ATTN_PV_DELAY + 1):
        if i + 1 < n_items:
            sc[i + 1] = scores(i + 1)
        if 0 <= i - ATTN_PV_DELAY < n_items:
            j = i - ATTN_PV_DELAY
            pv[j] = weighted_values(pm[j][0], j)
        if i < n_items:
            pm[i] = softmax(sc.pop(i), i)
        if 0 <= i - ATTN_PV_DELAY - 1 < n_items:
            j = i - ATTN_PV_DELAY - 1
            finish(pv.pop(j), pm.pop(j)[1], j)

    kd_ref[0:blk, :] = kd_ref[tile:tile + blk, :]
    vd_ref[0:blk, :] = vd_ref[tile:tile + blk, :]
    o_ref[...] = h + jnp.dot(att_ref[...], wo_ref[...], preferred_element_type=jnp.float32)


def _attn_layer(h, layer, j, g, w_qkv, b_qkv, sinks, w_o):
    bsz, seq, d = h.shape
    tile = SEQ_TILE
    nq = N_HEADS * HEAD_DIM
    nkv = N_KV_HEADS * HEAD_DIM
    qkv_dim = nq + 2 * nkv
    lanes = 2 * HEAD_DIM
    return pl.pallas_call(
        functools.partial(_attn_kernel, tile=tile),
        grid=(bsz, seq // tile),
        in_specs=[
            pl.BlockSpec(memory_space=pltpu.SMEM),
            _seq_tile_spec(tile, d),
            _resident((1, d), layer),
            _resident((d, qkv_dim), j),
            _resident((1, qkv_dim), j),
            _resident((nq, d), j),
        ],
        out_specs=_seq_tile_spec(tile, d),
        out_shape=jax.ShapeDtypeStruct(h.shape, h.dtype),
        scratch_shapes=[
            pltpu.VMEM((N_HEADS, tile, lanes), jnp.bfloat16),
            pltpu.VMEM((tile + ATTN_BLOCK, N_KV_HEADS * lanes), jnp.bfloat16),
            pltpu.VMEM((tile + ATTN_BLOCK, N_KV_HEADS * lanes), jnp.bfloat16),
            pltpu.VMEM((tile, nq), jnp.bfloat16),
            pltpu.VMEM((2, N_HEADS, ATTN_BLOCK, 2 * ATTN_BLOCK), jnp.float32),
        ],
        compiler_params=_compiler_params(),
        name="swa_sink_attention",
    )(sinks[j], h, g[:, None, :], w_qkv, b_qkv[:, None, :], w_o)


def kernel(x, pool_w, pool_b, pool_scale, attn_w_qkv, attn_b_qkv, attn_sinks, attn_w_o,
           norm_mix, norm_ffn, ffn_w_in, ffn_conv_w, ffn_conv_b, ffn_w_out, norm_f):
    depth = norm_mix.shape[0]
    w_in_bf16 = ffn_w_in.astype(jnp.bfloat16)
    w_out_bf16 = ffn_w_out.astype(jnp.bfloat16)
    pool_w_bf16 = pool_w.astype(jnp.bfloat16)
    w_qkv_bf16 = attn_w_qkv.astype(jnp.bfloat16)
    w_o_bf16 = attn_w_o.astype(jnp.bfloat16)
    h = x
    for i in range(depth):
        j = i // 2
        if i % 2 == 0:
            h = _pool_layer(h, i, j, norm_mix, pool_w_bf16, pool_b, pool_scale)
        else:
            h = _attn_layer(h, i, j, norm_mix, w_qkv_bf16, attn_b_qkv, attn_sinks, w_o_bf16)
        h = _ffn_layer(h, i, norm_ffn, w_in_bf16, ffn_conv_w, ffn_conv_b, w_out_bf16,
                       g_final=norm_f if i == depth - 1 else None)
    return h
```
